```python
import math
import jax, jax.numpy as jnp
from jax import lax
import numpy as np

D_MODEL = 1024
BATCH = 32
SEQ = 2048
DEPTH = 2

N_MIXERS = 2
N_A_LAYERS = (DEPTH + 1) // 2
N_B_LAYERS = DEPTH // 2
BRANCH = D_MODEL
CONFORMER_K = 31
SHORT_K = 3
ALPHA = (2.0 * DEPTH) ** 0.25
BETA = (8.0 * DEPTH) ** -0.25
LN_EPS = 1e-5

kernel_name = "hybrid_conformer_shortconv_deepnorm_adaln"


def _layer_norm(x, g, b):
    xf = x.astype(jnp.float32)
    mu = jnp.mean(xf, axis=-1, keepdims=True)
    var = jnp.mean(jnp.square(xf - mu), axis=-1, keepdims=True)
    y = (xf - mu) * lax.rsqrt(var + LN_EPS) * g.astype(jnp.float32) + b.astype(jnp.float32)
    return y.astype(x.dtype)


def _causal_depthwise_conv(h, w):
    k = w.shape[0]
    return lax.conv_general_dilated(
        h, w[:, None, :].astype(h.dtype),
        window_strides=(1,), padding=[(k - 1, 0)],
        dimension_numbers=("NWC", "WIO", "NWC"),
        feature_group_count=h.shape[-1])


def _conformer_mixer(u, w_in, b_in, conv_w, conv_b, ln_g, ln_b, w_out, b_out):
    p = jnp.einsum("bsd,de->bse", u, w_in) + b_in
    a, g, z = jnp.split(p, 3, axis=-1)
    h = a * jax.nn.sigmoid(g)
    h = _causal_depthwise_conv(h, conv_w) + conv_b
    h = jax.nn.silu(_layer_norm(h, ln_g, ln_b))
    return jnp.einsum("bse,ed->bsd", h * jax.nn.silu(z), w_out) + b_out


def _short_conv_mixer(u, w_in, conv_w, w_out):
    p = jnp.einsum("bsd,de->bse", u, w_in)
    bg, cg, v, z = jnp.split(p, 4, axis=-1)
    h = _causal_depthwise_conv(cg * v, conv_w)
    return jnp.einsum("bse,ed->bsd", bg * h * jax.nn.silu(z), w_out)


def setup_inputs(seed: int = 0) -> dict:
    key = jax.random.key(seed)
    ks = jax.random.split(key, 20)
    d, e = D_MODEL, BRANCH
    f32 = jnp.float32
    nrm = lambda k, shape, s: jax.random.normal(k, shape, f32) * s
    return {
        "x": nrm(ks[0], (BATCH, SEQ, d), 1.0),
        "c": nrm(ks[1], (BATCH, d), 1.0),
        "ada_w": nrm(ks[2], (DEPTH, d, 3 * d), 0.5 * d ** -0.5),
        "ada_b": nrm(ks[3], (DEPTH, 3 * d), 0.01),
        "a_w_in": nrm(ks[4], (N_A_LAYERS, d, 3 * e), d ** -0.5),
        "a_b_in": nrm(ks[5], (N_A_LAYERS, 3 * e), 0.01),
        "a_conv_w": nrm(ks[6], (N_A_LAYERS, CONFORMER_K, e), CONFORMER_K ** -0.5),
        "a_conv_b": nrm(ks[7], (N_A_LAYERS, e), 0.01),
        "a_ln_g": 1.0 + nrm(ks[8], (N_A_LAYERS, e), 0.02),
        "a_ln_b": nrm(ks[9], (N_A_LAYERS, e), 0.01),
        "a_w_out": nrm(ks[10], (N_A_LAYERS, e, d), BETA * e ** -0.5),
        "a_b_out": nrm(ks[11], (N_A_LAYERS, d), 0.01),
        "b_w_in": nrm(ks[12], (N_B_LAYERS, d, 4 * e), d ** -0.5),
        "b_conv_w": nrm(ks[13], (N_B_LAYERS, SHORT_K, e), SHORT_K ** -0.5),
        "b_w_out": nrm(ks[14], (N_B_LAYERS, e, d), BETA * e ** -0.5),
        "post_ln_g": 1.0 + nrm(ks[15], (DEPTH, d), 0.02),
        "post_ln_b": nrm(ks[16], (DEPTH, d), 0.01),
    }


def reference(x, c, ada_w, ada_b, a_w_in, a_b_in, a_conv_w, a_conv_b, a_ln_g, a_ln_b,
              a_w_out, a_b_out, b_w_in, b_conv_w, b_w_out, post_ln_g, post_ln_b):
    cond = jax.nn.silu(c)
    for l in range(DEPTH):
        mod = jnp.einsum("bd,de->be", cond, ada_w[l]) + ada_b[l]
        shift, scale, gate = jnp.split(mod[:, None, :], 3, axis=-1)
        u = x * (1.0 + scale) + shift
        i = l // N_MIXERS
        if l % N_MIXERS == 0:
            y = _conformer_mixer(u, a_w_in[i], a_b_in[i], a_conv_w[i], a_conv_b[i],
                                 a_ln_g[i], a_ln_b[i], a_w_out[i], a_b_out[i])
        else:
            y = _short_conv_mixer(u, b_w_in[i], b_conv_w[i], b_w_out[i])
        x = _layer_norm(ALPHA * x + gate * y, post_ln_g[l], post_ln_b[l])
    return x
```

```python
import functools

import jax
import jax.numpy as jnp
from jax import lax
from jax.experimental import pallas as pl
from jax.experimental.pallas import tpu as pltpu

DEPTH = 2
ALPHA = (2.0 * DEPTH) ** 0.25
LN_EPS = 1e-5

LANES = 128
SUBLANES = 8
SEQ_TILE = 512
HALO = 32
MATMUL_N = 256
ROW_CHUNK = 32
CONV_CHUNK = 128
VMEM_LIMIT = 52 * 1024 * 1024

F32 = jnp.float32
BF16 = jnp.bfloat16


def _sigmoid(v):
    return 1.0 / (1.0 + jnp.exp(-v))


def _silu(v):
    return v * _sigmoid(v)


def _mod_kernel(c_ref, w_ref, b_ref, o_ref):
    cond = _silu(c_ref[...])
    o_ref[...] = jnp.dot(cond, w_ref[...], preferred_element_type=F32,
                         precision=lax.Precision.HIGHEST) + b_ref[...]


def _modulation(c, ada_w, ada_b):
    depth, d, n = ada_w.shape
    b = c.shape[0]
    nt = 1024
    return pl.pallas_call(
        _mod_kernel,
        grid=(depth, n // nt),
        in_specs=[
            pl.BlockSpec((b, d), lambda l, j: (0, 0)),
            pl.BlockSpec((None, d, nt), lambda l, j: (l, 0, j)),
            pl.BlockSpec((None, 1, nt), lambda l, j: (l, 0, j)),
        ],
        out_specs=pl.BlockSpec((None, b, nt), lambda l, j: (l, 0, j)),
        out_shape=jax.ShapeDtypeStruct((depth, b, n), F32),
        name="adaln_mod",
    )(c, ada_w, ada_b.reshape(depth, 1, n))


def _modulate_to_bf16(x_ref, mod_ref, u_ref, d):
    shift = mod_ref[:, 0:d]
    scale = mod_ref[:, d:2 * d]
    u_ref[...] = (x_ref[...] * (1.0 + scale) + shift).astype(BF16)


def _out_proj_residual_norm(g_ref, w_out_ref, b_out, x_ref, mod_ref, pg_ref, pb_ref,
                            r_ref, o_ref, d):
    t = x_ref.shape[0]
    for nb in range(d // MATMUL_N):
        cols = slice(nb * MATMUL_N, (nb + 1) * MATMUL_N)
        y = jnp.dot(g_ref[...], w_out_ref[:, cols], preferred_element_type=F32)
        if b_out is not None:
            y = y + b_out[:, cols]
        gate = mod_ref[:, 2 * d + nb * MATMUL_N:2 * d + (nb + 1) * MATMUL_N]
        r_ref[:, cols] = ALPHA * x_ref[:, cols] + gate * y

    def norm_chunk(i, carry):
        rows = pl.ds(pl.multiple_of(i * ROW_CHUNK, ROW_CHUNK), ROW_CHUNK)
        r = r_ref[rows, :]
        mu = jnp.mean(r, axis=-1, keepdims=True)
        dev = r - mu
        var = jnp.mean(dev * dev, axis=-1, keepdims=True)
        o_ref[rows, :] = dev * lax.rsqrt(var + LN_EPS) * pg_ref[...] + pb_ref[...]
        return carry

    lax.fori_loop(0, t // ROW_CHUNK, norm_chunk, 0)


def _layer_a_kernel(x_ref, mod_ref, w_in_ref, b_in_ref, cw_ref, cb_ref, lg_ref, lb_ref,
                    w_out_ref, b_out_ref, pg_ref, pb_ref, o_ref,
                    u_ref, h_ref, z_ref, c_ref, g_ref, *, k_taps):
    t, d = x_ref.shape
    e = z_ref.shape[1]
    n_slabs = e // LANES
    first = HALO - (k_taps - 1)

    @pl.when(pl.program_id(1) == 0)
    def _():
        h_ref[:, 0:HALO, :] = jnp.zeros((n_slabs, HALO, LANES), F32)

    @pl.when(pl.program_id(1) > 0)
    def _():
        h_ref[:, 0:HALO, :] = h_ref[:, t:t + HALO, :]

    _modulate_to_bf16(x_ref, mod_ref, u_ref, d)

    for nb in range(e // MATMUL_N):
        c0 = nb * MATMUL_N
        u = u_ref[...]
        a = jnp.dot(u, w_in_ref[:, c0:c0 + MATMUL_N], preferred_element_type=F32)
        a = a + b_in_ref[:, c0:c0 + MATMUL_N]
        g = jnp.dot(u, w_in_ref[:, e + c0:e + c0 + MATMUL_N], preferred_element_type=F32)
        g = g + b_in_ref[:, e + c0:e + c0 + MATMUL_N]
        h = a * _sigmoid(g)
        for q in range(MATMUL_N // LANES):
            h_ref[c0 // LANES + q, HALO:HALO + t, :] = h[:, q * LANES:(q + 1) * LANES]
        z = jnp.dot(u, w_in_ref[:, 2 * e + c0:2 * e + c0 + MATMUL_N],
                    preferred_element_type=F32)
        z = z + b_in_ref[:, 2 * e + c0:2 * e + c0 + MATMUL_N]
        z_ref[:, c0:c0 + MATMUL_N] = _silu(z)

    def conv_slab(j, carry):
        lanes = pl.ds(pl.multiple_of(j * LANES, LANES), LANES)
        for rc in range(t // CONV_CHUNK):
            r0 = rc * CONV_CHUNK
            acc = jnp.zeros((CONV_CHUNK, LANES), F32) + cb_ref[:, lanes]
            for k in range(k_taps):
                acc = acc + cw_ref[j, k:k + 1, :] * h_ref[j, r0 + first + k:r0 + first + k + CONV_CHUNK, :]
            c_ref[r0:r0 + CONV_CHUNK, lanes] = acc
        return carry

    lax.fori_loop(0, n_slabs, conv_slab, 0)

    def norm_chunk(i, carry):
        rows = pl.ds(pl.multiple_of(i * ROW_CHUNK, ROW_CHUNK), ROW_CHUNK)
        cv = c_ref[rows, :]
        mu = jnp.mean(cv, axis=-1, keepdims=True)
        dev = cv - mu
        var = jnp.mean(dev * dev, axis=-1, keepdims=True)
        hn = dev * lax.rsqrt(var + LN_EPS) * lg_ref[...] + lb_ref[...]
        g_ref[rows, :] = (_silu(hn) * z_ref[rows, :]).astype(BF16)
        return carry

    lax.fori_loop(0, t // ROW_CHUNK, norm_chunk, 0)

    _out_proj_residual_norm(g_ref, w_out_ref, b_out_ref, x_ref, mod_ref, pg_ref, pb_ref,
                            c_ref, o_ref, d)


def _layer_a(x, mod, w_in, b_in, conv_w, conv_b, ln_g, ln_b, w_out, b_out, pg, pb):
    b, s, d = x.shape
    e = w_out.shape[0]
    k_taps = conv_w.shape[0]
    t = SEQ_TILE
    n_slabs = e // LANES
    cw = conv_w.reshape(k_taps, n_slabs, LANES).transpose(1, 0, 2)
    row = lambda v: v.reshape(1, -1)
    const = lambda shape: pl.BlockSpec(shape, lambda i, j: (0,) * len(shape))
    return pl.pallas_call(
        functools.partial(_layer_a_kernel, k_taps=k_taps),
        grid=(b, s // t),
        in_specs=[
            pl.BlockSpec((None, t, d), lambda i, j: (i, j, 0)),
            pl.BlockSpec((None, 1, 3 * d), lambda i, j: (i, 0, 0)),
            const((d, 3 * e)), const((1, 3 * e)),
            const((n_slabs, k_taps, LANES)), const((1, e)), const((1, e)), const((1, e)),
            const((e, d)), const((1, d)), const((1, d)), const((1, d)),
        ],
        out_specs=pl.BlockSpec((None, t, d), lambda i, j: (i, j, 0)),
        out_shape=jax.ShapeDtypeStruct((b, s, d), F32),
        scratch_shapes=[
            pltpu.VMEM((t, d), BF16),
            pltpu.VMEM((n_slabs, HALO + t, LANES), F32),
            pltpu.VMEM((t, e), F32),
            pltpu.VMEM((t, e), F32),
            pltpu.VMEM((t, e), BF16),
        ],
        compiler_params=pltpu.CompilerParams(
            dimension_semantics=("arbitrary", "arbitrary"),
            vmem_limit_bytes=VMEM_LIMIT),
        name="layer_a",
    )(x, mod.reshape(b, 1, 3 * d), w_in.astype(BF16), row(b_in), cw, row(conv_b),
      row(ln_g), row(ln_b), w_out.astype(BF16), row(b_out), row(pg), row(pb))


def _layer_b_kernel(x_ref, mod_ref, w_in_ref, cw_ref, w_out_ref, pg_ref, pb_ref, o_ref,
                    u_ref, cv_ref, g_ref, r_ref, *, k_taps):
    t, d = x_ref.shape
    e = g_ref.shape[1]
    first = SUBLANES - (k_taps - 1)

    @pl.when(pl.program_id(1) == 0)
    def _():
        cv_ref[0:SUBLANES, :] = jnp.zeros((SUBLANES, e), F32)

    @pl.when(pl.program_id(1) > 0)
    def _():
        cv_ref[0:SUBLANES, :] = cv_ref[t:t + SUBLANES, :]

    _modulate_to_bf16(x_ref, mod_ref, u_ref, d)

    for nb in range(e // MATMUL_N):
        c0 = nb * MATMUL_N
        cols = slice(c0, c0 + MATMUL_N)
        u = u_ref[...]
        cg = jnp.dot(u, w_in_ref[:, e + c0:e + c0 + MATMUL_N], preferred_element_type=F32)
        v = jnp.dot(u, w_in_ref[:, 2 * e + c0:2 * e + c0 + MATMUL_N],
                    preferred_element_type=F32)
        cv_ref[SUBLANES:SUBLANES + t, cols] = cg * v
        h = jnp.zeros((t, MATMUL_N), F32)
        for k in range(k_taps):
            h = h + cw_ref[k:k + 1, cols] * cv_ref[first + k:first + k + t, cols]
        bg = jnp.dot(u, w_in_ref[:, cols], preferred_element_type=F32)
        z = jnp.dot(u, w_in_ref[:, 3 * e + c0:3 * e + c0 + MATMUL_N],
                    preferred_element_type=F32)
        g_ref[:, cols] = (bg * h * _silu(z)).astype(BF16)

    _out_proj_residual_norm(g_ref, w_out_ref, None, x_ref, mod_ref, pg_ref, pb_ref,
                            r_ref, o_ref, d)


def _layer_b(x, mod, w_in, conv_w, w_out, pg, pb):
    b, s, d = x.shape
    e = w_out.shape[0]
    k_taps = conv_w.shape[0]
    t = SEQ_TILE
    row = lambda v: v.reshape(1, -1)
    const = lambda shape: pl.BlockSpec(shape, lambda i, j: (0,) * len(shape))
    return pl.pallas_call(
        functools.partial(_layer_b_kernel, k_taps=k_taps),
        grid=(b, s // t),
        in_specs=[
            pl.BlockSpec((None, t, d), lambda i, j: (i, j, 0)),
            pl.BlockSpec((None, 1, 3 * d), lambda i, j: (i, 0, 0)),
            const((d, 4 * e)), const((k_taps, e)), const((e, d)),
            const((1, d)), const((1, d)),
        ],
        out_specs=pl.BlockSpec((None, t, d), lambda i, j: (i, j, 0)),
        out_shape=jax.ShapeDtypeStruct((b, s, d), F32),
        scratch_shapes=[
            pltpu.VMEM((t, d), BF16),
            pltpu.VMEM((SUBLANES + t, e), F32),
            pltpu.VMEM((t, e), BF16),
            pltpu.VMEM((t, d), F32),
        ],
        compiler_params=pltpu.CompilerParams(
            dimension_semantics=("arbitrary", "arbitrary"),
            vmem_limit_bytes=VMEM_LIMIT),
        name="layer_b",
    )(x, mod.reshape(b, 1, 3 * d), w_in.astype(BF16), conv_w, w_out.astype(BF16),
      row(pg), row(pb))


def kernel(x, c, ada_w, ada_b, a_w_in, a_b_in, a_conv_w, a_conv_b, a_ln_g, a_ln_b,
           a_w_out, a_b_out, b_w_in, b_conv_w, b_w_out, post_ln_g, post_ln_b):
    assert DEPTH == ada_w.shape[0] == 2 and a_w_in.shape[0] == 1 and b_w_in.shape[0] == 1
    assert a_conv_w.shape[1] - 1 <= HALO and b_conv_w.shape[1] - 1 <= SUBLANES
    assert x.shape[1] % SEQ_TILE == 0
    mod = _modulation(c, ada_w, ada_b)
    x = _layer_a(x, mod[0], a_w_in[0], a_b_in[0], a_conv_w[0], a_conv_b[0], a_ln_g[0],
                 a_ln_b[0], a_w_out[0], a_b_out[0], post_ln_g[0], post_ln_b[0])
    x = _layer_b(x, mod[1], b_w_in[0], b_conv_w[0], b_w_out[0], post_ln_g[1], post_ln_b[1])
    return x
```

```python
import functools
import math

import jax
import jax.numpy as jnp
from jax import lax
from jax.experimental import pallas as pl
from jax.experimental.pallas import tpu as pltpu

DEPTH = 2
ALPHA = (2.0 * DEPTH) ** 0.25
LN_EPS = 1e-5
NEG_LOG2_E = -math.log2(math.e)

LANES = 128
SUBLANES = 8
SEQ_TILE = 512
HALO = 32
MATMUL_N = 256
OUT_ROWS = 128
NORM_ROWS = 8
CONV_ROWS = 64
VMEM_LIMIT = 52 * 1024 * 1024

F32 = jnp.float32
BF16 = jnp.bfloat16


def _sigmoid(v):
    return 1.0 / (1.0 + jnp.exp2(v * NEG_LOG2_E))


def _silu(v):
    return v * _sigmoid(v)


def _layer_norm(v, gamma, beta):
    inv_n = 1.0 / v.shape[-1]
    mu = jnp.sum(v, axis=-1, keepdims=True) * inv_n
    dev = v - mu
    var = jnp.sum(dev * dev, axis=-1, keepdims=True) * inv_n
    return dev * lax.rsqrt(var + LN_EPS) * gamma + beta


def _mod_kernel(c_ref, w_ref, b_ref, o_ref):
    cond = _silu(c_ref[...])
    o_ref[...] = jnp.dot(cond, w_ref[...], preferred_element_type=F32,
                         precision=lax.Precision.HIGHEST) + b_ref[...]


def _modulation(c, ada_w, ada_b):
    depth, d, n = ada_w.shape
    b = c.shape[0]
    nt = 1024
    return pl.pallas_call(
        _mod_kernel,
        grid=(depth, n // nt),
        in_specs=[
            pl.BlockSpec((b, d), lambda l, j: (0, 0)),
            pl.BlockSpec((None, d, nt), lambda l, j: (l, 0, j)),
            pl.BlockSpec((None, 1, nt), lambda l, j: (l, 0, j)),
        ],
        out_specs=pl.BlockSpec((None, b, nt), lambda l, j: (l, 0, j)),
        out_shape=jax.ShapeDtypeStruct((depth, b, n), F32),
        name="adaln_mod",
    )(c, ada_w, ada_b.reshape(depth, 1, n))


def _modulate_to_bf16(x_ref, mod_ref, u_ref, d):
    shift = mod_ref[:, 0:d]
    scale = mod_ref[:, d:2 * d]
    u_ref[...] = (x_ref[...] * (1.0 + scale) + shift).astype(BF16)


def _out_proj_residual_norm(g_ref, w_out_ref, b_out_ref, x_ref, mod_ref, pg_ref, pb_ref,
                            r_ref, o_ref, d):
    t = x_ref.shape[0]
    gate = mod_ref[:, 2 * d:3 * d]
    for r0 in range(0, t, OUT_ROWS):
        rows = slice(r0, r0 + OUT_ROWS)
        y = jnp.dot(g_ref[rows, :], w_out_ref[...], preferred_element_type=F32)
        if b_out_ref is not None:
            y = y + b_out_ref[...]
        r_ref[rows, :] = ALPHA * x_ref[rows, :] + gate * y
        for q0 in range(r0, r0 + OUT_ROWS, NORM_ROWS):
            grp = slice(q0, q0 + NORM_ROWS)
            o_ref[grp, :] = _layer_norm(r_ref[grp, :], pg_ref[...], pb_ref[...])


def _layer_a_kernel(x_ref, mod_ref, w_in_ref, b_in_ref, cw_ref, cb_ref, lg_ref, lb_ref,
                    w_out_ref, b_out_ref, pg_ref, pb_ref, o_ref,
                    u_ref, h_ref, z_ref, c_ref, g_ref, *, k_taps):
    t, d = x_ref.shape
    e = z_ref.shape[1]
    n_slabs = e // LANES
    first = HALO - (k_taps - 1)

    @pl.when(pl.program_id(1) == 0)
    def _():
        h_ref[:, 0:HALO, :] = jnp.zeros((n_slabs, HALO, LANES), F32)

    @pl.when(pl.program_id(1) > 0)
    def _():
        h_ref[:, 0:HALO, :] = h_ref[:, t:t + HALO, :]

    _modulate_to_bf16(x_ref, mod_ref, u_ref, d)

    for c0 in range(0, e, MATMUL_N):
        u = u_ref[...]
        a = jnp.dot(u, w_in_ref[:, c0:c0 + MATMUL_N], preferred_element_type=F32)
        a = a + b_in_ref[:, c0:c0 + MATMUL_N]
        g = jnp.dot(u, w_in_ref[:, e + c0:e + c0 + MATMUL_N], preferred_element_type=F32)
        g = g + b_in_ref[:, e + c0:e + c0 + MATMUL_N]
        h = a * _sigmoid(g)
        for q in range(MATMUL_N // LANES):
            h_ref[c0 // LANES + q, HALO:HALO + t, :] = h[:, q * LANES:(q + 1) * LANES]
        z = jnp.dot(u, w_in_ref[:, 2 * e + c0:2 * e + c0 + MATMUL_N],
                    preferred_element_type=F32)
        z = z + b_in_ref[:, 2 * e + c0:2 * e + c0 + MATMUL_N]
        z_ref[:, c0:c0 + MATMUL_N] = _silu(z)

        for j in range(c0 // LANES, (c0 + MATMUL_N) // LANES):
            lanes = slice(j * LANES, (j + 1) * LANES)
            for r0 in range(0, t, CONV_ROWS):
                acc = cb_ref[:, lanes] + cw_ref[j, 0:1, :] * h_ref[j, r0 + first:r0 + first + CONV_ROWS, :]
                for k in range(1, k_taps):
                    acc = acc + cw_ref[j, k:k + 1, :] * h_ref[j, r0 + first + k:r0 + first + k + CONV_ROWS, :]
                c_ref[r0:r0 + CONV_ROWS, lanes] = acc

    for q0 in range(0, t, NORM_ROWS):
        grp = slice(q0, q0 + NORM_ROWS)
        hn = _layer_norm(c_ref[grp, :], lg_ref[...], lb_ref[...])
        g_ref[grp, :] = (_silu(hn) * z_ref[grp, :]).astype(BF16)

    _out_proj_residual_norm(g_ref, w_out_ref, b_out_ref, x_ref, mod_ref, pg_ref, pb_ref,
                            c_ref, o_ref, d)


def _layer_a(x, mod, w_in, b_in, conv_w, conv_b, ln_g, ln_b, w_out, b_out, pg, pb):
    b, s, d = x.shape
    e = w_out.shape[0]
    k_taps = conv_w.shape[0]
    t = SEQ_TILE
    n_slabs = e // LANES
    cw = conv_w.reshape(k_taps, n_slabs, LANES).transpose(1, 0, 2)
    row = lambda v: v.reshape(1, -1)
    const = lambda shape: pl.BlockSpec(shape, lambda i, j: (0,) * len(shape))
    return pl.pallas_call(
        functools.partial(_layer_a_kernel, k_taps=k_taps),
        grid=(b, s // t),
        in_specs=[
            pl.BlockSpec((None, t, d), lambda i, j: (i, j, 0)),
            pl.BlockSpec((None, 1, 3 * d), lambda i, j: (i, 0, 0)),
            const((d, 3 * e)), const((1, 3 * e)),
            const((n_slabs, k_taps, LANES)), const((1, e)), const((1, e)), const((1, e)),
            const((e, d)), const((1, d)), const((1, d)), const((1, d)),
        ],
        out_specs=pl.BlockSpec((None, t, d), lambda i, j: (i, j, 0)),
        out_shape=jax.ShapeDtypeStruct((b, s, d), F32),
        scratch_shapes=[
            pltpu.VMEM((t, d), BF16),
            pltpu.VMEM((n_slabs, HALO + t, LANES), F32),
            pltpu.VMEM((t, e), F32),
            pltpu.VMEM((t, e), F32),
            pltpu.VMEM((t, e), BF16),
        ],
        compiler_params=pltpu.CompilerParams(
            dimension_semantics=("arbitrary", "arbitrary"),
            vmem_limit_bytes=VMEM_LIMIT),
        name="layer_a",
    )(x, mod.reshape(b, 1, 3 * d), w_in.astype(BF16), row(b_in), cw, row(conv_b),
      row(ln_g), row(ln_b), w_out.astype(BF16), row(b_out), row(pg), row(pb))


def _layer_b_kernel(x_ref, mod_ref, w_in_ref, cw_ref, w_out_ref, pg_ref, pb_ref, o_ref,
                    u_ref, cv_ref, g_ref, r_ref, *, k_taps):
    t, d = x_ref.shape
    e = g_ref.shape[1]
    first = SUBLANES - (k_taps - 1)

    @pl.when(pl.program_id(1) == 0)
    def _():
        cv_ref[0:SUBLANES, :] = jnp.zeros((SUBLANES, e), F32)

    @pl.when(pl.program_id(1) > 0)
    def _():
        cv_ref[0:SUBLANES, :] = cv_ref[t:t + SUBLANES, :]

    _modulate_to_bf16(x_ref, mod_ref, u_ref, d)

    for c0 in range(0, e, MATMUL_N):
        cols = slice(c0, c0 + MATMUL_N)
        u = u_ref[...]
        cg = jnp.dot(u, w_in_ref[:, e + c0:e + c0 + MATMUL_N], preferred_element_type=F32)
        v = jnp.dot(u, w_in_ref[:, 2 * e + c0:2 * e + c0 + MATMUL_N],
                    preferred_element_type=F32)
        cv_ref[SUBLANES:SUBLANES + t, cols] = cg * v
        h = cw_ref[0:1, cols] * cv_ref[first:first + t, cols]
        for k in range(1, k_taps):
            h = h + cw_ref[k:k + 1, cols] * cv_ref[first + k:first + k + t, cols]
        bg = jnp.dot(u, w_in_ref[:, cols], preferred_element_type=F32)
        z = jnp.dot(u, w_in_ref[:, 3 * e + c0:3 * e + c0 + MATMUL_N],
                    preferred_element_type=F32)
        g_ref[:, cols] = (bg * h * _silu(z)).astype(BF16)

    _out_proj_residual_norm(g_ref, w_out_ref, None, x_ref, mod_ref, pg_ref, pb_ref,
                            r_ref, o_ref, d)


def _layer_b(x, mod, w_in, conv_w, w_out, pg, pb):
    b, s, d = x.shape
    e = w_out.shape[0]
    k_taps = conv_w.shape[0]
    t = SEQ_TILE
    row = lambda v: v.reshape(1, -1)
    const = lambda shape: pl.BlockSpec(shape, lambda i, j: (0,) * len(shape))
    return pl.pallas_call(
        functools.partial(_layer_b_kernel, k_taps=k_taps),
        grid=(b, s // t),
        in_specs=[
            pl.BlockSpec((None, t, d), lambda i, j: (i, j, 0)),
            pl.BlockSpec((None, 1, 3 * d), lambda i, j: (i, 0, 0)),
            const((d, 4 * e)), const((k_taps, e)), const((e, d)),
            const((1, d)), const((1, d)),
        ],
        out_specs=pl.BlockSpec((None, t, d), lambda i, j: (i, j, 0)),
        out_shape=jax.ShapeDtypeStruct((b, s, d), F32),
        scratch_shapes=[
            pltpu.VMEM((t, d), BF16),
            pltpu.VMEM((SUBLANES + t, e), F32),
            pltpu.VMEM((t, e), BF16),
            pltpu.VMEM((t, d), F32),
        ],
        compiler_params=pltpu.CompilerParams(
            dimension_semantics=("arbitrary", "arbitrary"),
            vmem_limit_bytes=VMEM_LIMIT),
        name="layer_b",
    )(x, mod.reshape(b, 1, 3 * d), w_in.astype(BF16), conv_w, w_out.astype(BF16),
      row(pg), row(pb))


def kernel(x, c, ada_w, ada_b, a_w_in, a_b_in, a_conv_w, a_conv_b, a_ln_g, a_ln_b,
           a_w_out, a_b_out, b_w_in, b_conv_w, b_w_out, post_ln_g, post_ln_b):
    assert DEPTH == ada_w.shape[0] == 2 and a_w_in.shape[0] == 1 and b_w_in.shape[0] == 1
    assert a_conv_w.shape[1] - 1 <= HALO and b_conv_w.shape[1] - 1 <= SUBLANES
    assert x.shape[1] % SEQ_TILE == 0
    mod = _modulation(c, ada_w, ada_b)
    x = _layer_a(x, mod[0], a_w_in[0], a_b_in[0], a_conv_w[0], a_conv_b[0], a_ln_g[0],
                 a_ln_b[0], a_w_out[0], a_b_out[0], post_ln_g[0], post_ln_b[0])
    x = _layer_b(x, mod[1], b_w_in[0], b_conv_w[0], b_w_out[0], post_ln_g[1], post_ln_b[1])
    return x
```

```python
import functools
import math

import jax
import jax.numpy as jnp
from jax import lax
from jax.experimental import pallas as pl
from jax.experimental.pallas import tpu as pltpu

DEPTH = 2
ALPHA = (2.0 * DEPTH) ** 0.25
LN_EPS = 1e-5
NEG_LOG2_E = -math.log2(math.e)

LANES = 128
SUBLANES = 8
SEQ_TILE = 512
HALO_A = 32
HALO_B = SUBLANES
MATMUL_N = 256
NORM_ROWS = 8
CONV_ROWS = 64
VMEM_LIMIT = 58 * 1024 * 1024

F32 = jnp.float32
BF16 = jnp.bfloat16


def _sigmoid(v):
    return 1.0 / (1.0 + jnp.exp2(v * NEG_LOG2_E))


def _silu(v):
    return v * _sigmoid(v)


def _layer_norm(v, gamma, beta):
    inv_n = 1.0 / v.shape[-1]
    mu = jnp.sum(v, axis=-1, keepdims=True) * inv_n
    dev = v - mu
    var = jnp.sum(dev * dev, axis=-1, keepdims=True) * inv_n
    return dev * lax.rsqrt(var + LN_EPS) * gamma + beta


def _dot(a, b):
    return jnp.dot(a, b, preferred_element_type=F32)


def _mod_kernel(c_ref, w_ref, b_ref, o_ref):
    cond = _silu(c_ref[...])
    o_ref[...] = jnp.dot(cond, w_ref[...], preferred_element_type=F32,
                         precision=lax.Precision.HIGHEST) + b_ref[...]


def _modulation(c, ada_w, ada_b):
    depth, d, n = ada_w.shape
    b = c.shape[0]
    nt = 1024
    return pl.pallas_call(
        _mod_kernel,
        grid=(depth, n // nt),
        in_specs=[
            pl.BlockSpec((b, d), lambda l, j: (0, 0)),
            pl.BlockSpec((None, d, nt), lambda l, j: (l, 0, j)),
            pl.BlockSpec((None, 1, nt), lambda l, j: (l, 0, j)),
        ],
        out_specs=pl.BlockSpec((None, b, nt), lambda l, j: (l, 0, j)),
        out_shape=jax.ShapeDtypeStruct((depth, b, n), F32),
        name="adaln_mod",
    )(c, ada_w, ada_b.reshape(depth, 1, n))


def _conv_slab(src_ref, j, first, w_ref, bias, rows, k_taps):
    r0, n = rows
    acc = w_ref[j, 0:1, :] * src_ref[j, r0 + first:r0 + first + n, :]
    if bias is not None:
        acc = acc + bias
    for k in range(1, k_taps):
        acc = acc + w_ref[j, k:k + 1, :] * src_ref[j, r0 + first + k:r0 + first + k + n, :]
    return acc


def _trunk_kernel(x_ref, mod_a_ref, mod_b_ref,
                  a_w_in_ref, a_b_in_ref, a_cw_ref, a_cb_ref, a_lg_ref, a_lb_ref,
                  a_w_out_ref, a_b_out_ref, a_pg_ref, a_pb_ref,
                  b_w_in_ref, b_cw_ref, b_w_out_ref, b_pg_ref, b_pb_ref,
                  o_ref,
                  ua_ref, h_ref, z_ref, c_ref, ga_ref, x1_ref,
                  ub_ref, cv_ref, t1_ref, gb_ref, rb_ref,
                  *, k_a, k_b, tiles_per_seq):
    i = pl.program_id(0)
    t, d = x_ref.shape
    e = z_ref.shape[1]
    n_slabs = e // LANES
    first_a = HALO_A - (k_a - 1)
    first_b = HALO_B - (k_b - 1)
    a_w_in = a_w_in_ref.bitcast(BF16)
    a_w_out = a_w_out_ref.bitcast(BF16)
    b_w_in = b_w_in_ref.bitcast(BF16)
    b_w_out = b_w_out_ref.bitcast(BF16)

    a_starts = lax.rem(i, tiles_per_seq) == 0
    b_starts = jnp.logical_or(i == 0, lax.rem(i + tiles_per_seq - 1, tiles_per_seq) == 0)

    @pl.when(a_starts)
    def _():
        h_ref[:, 0:HALO_A, :] = jnp.zeros((n_slabs, HALO_A, LANES), F32)

    @pl.when(jnp.logical_not(a_starts))
    def _():
        h_ref[:, 0:HALO_A, :] = h_ref[:, t:t + HALO_A, :]

    @pl.when(b_starts)
    def _():
        cv_ref[:, 0:HALO_B, :] = jnp.zeros((n_slabs, HALO_B, LANES), F32)

    @pl.when(jnp.logical_not(b_starts))
    def _():
        cv_ref[:, 0:HALO_B, :] = cv_ref[:, t:t + HALO_B, :]

    @pl.when(i == 0)
    def _():
        x1_ref[...] = jnp.zeros(x1_ref.shape, F32)

    x1 = x1_ref[...]
    ub_ref[...] = (x1 * (1.0 + mod_b_ref[:, d:2 * d]) + mod_b_ref[:, 0:d]).astype(BF16)
    rb_ref[...] = ALPHA * x1

    ua_ref[...] = (x_ref[...] * (1.0 + mod_a_ref[:, d:2 * d]) + mod_a_ref[:, 0:d]).astype(BF16)

    for c0 in range(0, e, MATMUL_N):
        cols = slice(c0, c0 + MATMUL_N)
        slabs = range(c0 // LANES, (c0 + MATMUL_N) // LANES)

        ua = ua_ref[...]
        a = _dot(ua, a_w_in[:, c0:c0 + MATMUL_N]) + a_b_in_ref[:, c0:c0 + MATMUL_N]
        g = _dot(ua, a_w_in[:, e + c0:e + c0 + MATMUL_N]) + a_b_in_ref[:, e + c0:e + c0 + MATMUL_N]
        h = a * _sigmoid(g)
        for q, j in enumerate(slabs):
            h_ref[j, HALO_A:HALO_A + t, :] = h[:, q * LANES:(q + 1) * LANES]
        z = _dot(ua, a_w_in[:, 2 * e + c0:2 * e + c0 + MATMUL_N])
        z = z + a_b_in_ref[:, 2 * e + c0:2 * e + c0 + MATMUL_N]
        z_ref[:, cols] = _silu(z)
        for j in slabs:
            lanes = slice(j * LANES, (j + 1) * LANES)
            for r0 in range(0, t, CONV_ROWS):
                c_ref[r0:r0 + CONV_ROWS, lanes] = _conv_slab(
                    h_ref, j, first_a, a_cw_ref, a_cb_ref[:, lanes], (r0, CONV_ROWS), k_a)

        ub = ub_ref[...]
        cg = _dot(ub, b_w_in[:, e + c0:e + c0 + MATMUL_N])
        v = _dot(ub, b_w_in[:, 2 * e + c0:2 * e + c0 + MATMUL_N])
        cv = cg * v
        for q, j in enumerate(slabs):
            cv_ref[j, HALO_B:HALO_B + t, :] = cv[:, q * LANES:(q + 1) * LANES]
        bg = _dot(ub, b_w_in[:, c0:c0 + MATMUL_N])
        zb = _dot(ub, b_w_in[:, 3 * e + c0:3 * e + c0 + MATMUL_N])
        t1_ref[:, cols] = bg * _silu(zb)
        for j in slabs:
            lanes = slice(j * LANES, (j + 1) * LANES)
            for r0 in range(0, t, CONV_ROWS):
                hb = _conv_slab(cv_ref, j, first_b, b_cw_ref, None, (r0, CONV_ROWS), k_b)
                gb_ref[r0:r0 + CONV_ROWS, lanes] = (t1_ref[r0:r0 + CONV_ROWS, lanes] * hb).astype(BF16)

    for q0 in range(0, t, NORM_ROWS):
        grp = slice(q0, q0 + NORM_ROWS)
        hn = _layer_norm(c_ref[grp, :], a_lg_ref[...], a_lb_ref[...])
        ga_ref[grp, :] = (_silu(hn) * z_ref[grp, :]).astype(BF16)

    for c0 in range(0, d, MATMUL_N):
        cols = slice(c0, c0 + MATMUL_N)
        yb = _dot(gb_ref[...], b_w_out[:, cols])
        rb_ref[:, cols] = rb_ref[:, cols] + mod_b_ref[:, 2 * d + c0:2 * d + c0 + MATMUL_N] * yb
    for c0 in range(0, d, MATMUL_N):
        cols = slice(c0, c0 + MATMUL_N)
        ya = _dot(ga_ref[...], a_w_out[:, cols]) + a_b_out_ref[:, cols]
        c_ref[:, cols] = ALPHA * x_ref[:, cols] + mod_a_ref[:, 2 * d + c0:2 * d + c0 + MATMUL_N] * ya

    for q0 in range(0, t, NORM_ROWS):
        grp = slice(q0, q0 + NORM_ROWS)
        o_ref[grp, :] = _layer_norm(rb_ref[grp, :], b_pg_ref[...], b_pb_ref[...])
    for q0 in range(0, t, NORM_ROWS):
        grp = slice(q0, q0 + NORM_ROWS)
        x1_ref[grp, :] = _layer_norm(c_ref[grp, :], a_pg_ref[...], a_pb_ref[...])


def _pack_bf16_rows(w):
    k, n = w.shape
    pairs = jnp.swapaxes(w.astype(BF16).reshape(k // 2, 2, n), 1, 2)
    return lax.bitcast_convert_type(pairs, jnp.uint32)


def _slab_taps(conv_w):
    k_taps, e = conv_w.shape
    return conv_w.reshape(k_taps, e // LANES, LANES).transpose(1, 0, 2)


def _trunk(x, mod, a_w_in, a_b_in, a_conv_w, a_conv_b, a_ln_g, a_ln_b, a_w_out, a_b_out,
           b_w_in, b_conv_w, b_w_out, post_ln_g, post_ln_b):
    b, s, d = x.shape
    e = a_w_out.shape[0]
    k_a, k_b = a_conv_w.shape[0], b_conv_w.shape[0]
    t = SEQ_TILE
    tiles_per_seq = s // t
    n_tiles = b * tiles_per_seq
    n_slabs = e // LANES
    row = lambda v: v.reshape(1, -1)

    def resident(arr):
        return pl.BlockSpec(arr.shape, lambda i: (0,) * arr.ndim, pipeline_mode=pl.Buffered(1))

    a_tile = lambda i: jnp.minimum(i, n_tiles - 1)
    b_tile = lambda i: jnp.maximum(i - 1, 0)
    tile_spec = lambda tile: pl.BlockSpec(
        (None, t, d), lambda i: (tile(i) // tiles_per_seq, tile(i) % tiles_per_seq, 0))
    mod_spec = lambda tile: pl.BlockSpec((None, 1, 3 * d), lambda i: (tile(i) // tiles_per_seq, 0, 0))

    consts = [
        _pack_bf16_rows(a_w_in), row(a_b_in), _slab_taps(a_conv_w), row(a_conv_b),
        row(a_ln_g), row(a_ln_b), _pack_bf16_rows(a_w_out), row(a_b_out),
        row(post_ln_g[0]), row(post_ln_b[0]),
        _pack_bf16_rows(b_w_in), _slab_taps(b_conv_w), _pack_bf16_rows(b_w_out),
        row(post_ln_g[1]), row(post_ln_b[1]),
    ]
    return pl.pallas_call(
        functools.partial(_trunk_kernel, k_a=k_a, k_b=k_b, tiles_per_seq=tiles_per_seq),
        grid=(n_tiles + 1,),
        in_specs=[tile_spec(a_tile), mod_spec(a_tile), mod_spec(b_tile)]
                 + [resident(arr) for arr in consts],
        out_specs=tile_spec(b_tile),
        out_shape=jax.ShapeDtypeStruct((b, s, d), F32),
        scratch_shapes=[
            pltpu.VMEM((t, d), BF16),
            pltpu.VMEM((n_slabs, HALO_A + t, LANES), F32),
            pltpu.VMEM((t, e), F32),
            pltpu.VMEM((t, e), F32),
            pltpu.VMEM((t, e), BF16),
            pltpu.VMEM((t, d), F32),
            pltpu.VMEM((t, d), BF16),
            pltpu.VMEM((n_slabs, HALO_B + t, LANES), F32),
            pltpu.VMEM((t, e), F32),
            pltpu.VMEM((t, e), BF16),
            pltpu.VMEM((t, d), F32),
        ],
        compiler_params=pltpu.CompilerParams(
            dimension_semantics=("arbitrary",),
            vmem_limit_bytes=VMEM_LIMIT),
        name="trunk",
    )(x, mod[0].reshape(b, 1, 3 * d), mod[1].reshape(b, 1, 3 * d), *consts)


def kernel(x, c, ada_w, ada_b, a_w_in, a_b_in, a_conv_w, a_conv_b, a_ln_g, a_ln_b,
           a_w_out, a_b_out, b_w_in, b_conv_w, b_w_out, post_ln_g, post_ln_b):
    assert DEPTH == ada_w.shape[0] == 2 and a_w_in.shape[0] == 1 and b_w_in.shape[0] == 1
    assert a_conv_w.shape[1] - 1 <= HALO_A and b_conv_w.shape[1] - 1 <= HALO_B
    assert x.shape[1] % SEQ_TILE == 0
    mod = _modulation(c, ada_w, ada_b)
    return _trunk(x, mod, a_w_in[0], a_b_in[0], a_conv_w[0], a_conv_b[0], a_ln_g[0], a_ln_b[0],
                  a_w_out[0], a_b_out[0], b_w_in[0], b_conv_w[0], b_w_out[0],
                  post_ln_g, post_ln_b)
```

```python
import functools
import math

import jax
import jax.numpy as jnp
from jax import lax
from jax.experimental import pallas as pl
from jax.experimental.pallas import tpu as pltpu

DEPTH = 2
ALPHA = (2.0 * DEPTH) ** 0.25
LN_EPS = 1e-5
NEG_LOG2_E = -math.log2(math.e)

LANES = 128
SUBLANES = 8
SEQ_TILE = 512
HALO_A = 32
HALO_B = SUBLANES
MATMUL_N = 256
OUT_ROWS = 128
NORM_ROWS = 8
CONV_ROWS = 64
VMEM_LIMIT = 58 * 1024 * 1024

F32 = jnp.float32
BF16 = jnp.bfloat16


def _sigmoid(v):
    return 1.0 / (1.0 + jnp.exp2(v * NEG_LOG2_E))


def _silu(v):
    return v * _sigmoid(v)


def _layer_norm(v, gamma, beta):
    inv_n = 1.0 / v.shape[-1]
    mu = jnp.sum(v, axis=-1, keepdims=True) * inv_n
    dev = v - mu
    var = jnp.sum(dev * dev, axis=-1, keepdims=True) * inv_n
    return dev * lax.rsqrt(var + LN_EPS) * gamma + beta


def _dot(a, b):
    return jnp.dot(a, b, preferred_element_type=F32)


def _mod_kernel(c_ref, w_ref, b_ref, o_ref):
    cond = _silu(c_ref[...])
    o_ref[...] = jnp.dot(cond, w_ref[...], preferred_element_type=F32,
                         precision=lax.Precision.HIGHEST) + b_ref[...]


def _modulation(c, ada_w, ada_b):
    depth, d, n = ada_w.shape
    b = c.shape[0]
    nt = 1024
    return pl.pallas_call(
        _mod_kernel,
        grid=(depth, n // nt),
        in_specs=[
            pl.BlockSpec((b, d), lambda l, j: (0, 0)),
            pl.BlockSpec((None, d, nt), lambda l, j: (l, 0, j)),
            pl.BlockSpec((None, 1, nt), lambda l, j: (l, 0, j)),
        ],
        out_specs=pl.BlockSpec((None, b, nt), lambda l, j: (l, 0, j)),
        out_shape=jax.ShapeDtypeStruct((depth, b, n), F32),
        name="adaln_mod",
    )(c, ada_w, ada_b.reshape(depth, 1, n))


def _conv_slab(src_ref, j, first, w_ref, bias, rows, k_taps):
    r0, n = rows
    acc = w_ref[j, 0:1, :] * src_ref[j, r0 + first:r0 + first + n, :]
    if bias is not None:
        acc = acc + bias
    for k in range(1, k_taps):
        acc = acc + w_ref[j, k:k + 1, :] * src_ref[j, r0 + first + k:r0 + first + k + n, :]
    return acc


def _trunk_kernel(x_ref, mod_a_ref, mod_b_ref,
                  a_w_in_ref, a_b_in_ref, a_cw_ref, a_cb_ref, a_lg_ref, a_lb_ref,
                  a_w_out_ref, a_b_out_ref, a_pg_ref, a_pb_ref,
                  b_w_in_ref, b_cw_ref, b_w_out_ref, b_pg_ref, b_pb_ref,
                  o_ref,
                  ua_ref, h_ref, z_ref, c_ref, ga_ref, x1_ref,
                  ub_ref, cv_ref, t1_ref, gb_ref, rb_ref,
                  *, k_a, k_b, tiles_per_seq):
    i = pl.program_id(0)
    t, d = x_ref.shape
    e = z_ref.shape[1]
    n_slabs = e // LANES
    first_a = HALO_A - (k_a - 1)
    first_b = HALO_B - (k_b - 1)
    a_w_in = a_w_in_ref.bitcast(BF16)
    a_w_out = a_w_out_ref.bitcast(BF16)
    b_w_in = b_w_in_ref.bitcast(BF16)
    b_w_out = b_w_out_ref.bitcast(BF16)

    a_starts = lax.rem(i, tiles_per_seq) == 0
    b_starts = jnp.logical_or(i == 0, lax.rem(i + tiles_per_seq - 1, tiles_per_seq) == 0)

    @pl.when(a_starts)
    def _():
        h_ref[:, 0:HALO_A, :] = jnp.zeros((n_slabs, HALO_A, LANES), F32)

    @pl.when(jnp.logical_not(a_starts))
    def _():
        h_ref[:, 0:HALO_A, :] = h_ref[:, t:t + HALO_A, :]

    @pl.when(b_starts)
    def _():
        cv_ref[:, 0:HALO_B, :] = jnp.zeros((n_slabs, HALO_B, LANES), F32)

    @pl.when(jnp.logical_not(b_starts))
    def _():
        cv_ref[:, 0:HALO_B, :] = cv_ref[:, t:t + HALO_B, :]

    @pl.when(i == 0)
    def _():
        x1_ref[...] = jnp.zeros(x1_ref.shape, F32)

    x1 = x1_ref[...]
    ub_ref[...] = (x1 * (1.0 + mod_b_ref[:, d:2 * d]) + mod_b_ref[:, 0:d]).astype(BF16)
    rb_ref[...] = ALPHA * x1

    ua_ref[...] = (x_ref[...] * (1.0 + mod_a_ref[:, d:2 * d]) + mod_a_ref[:, 0:d]).astype(BF16)

    for c0 in range(0, e, MATMUL_N):
        cols = slice(c0, c0 + MATMUL_N)
        slabs = range(c0 // LANES, (c0 + MATMUL_N) // LANES)

        ua = ua_ref[...]
        a = _dot(ua, a_w_in[:, c0:c0 + MATMUL_N]) + a_b_in_ref[:, c0:c0 + MATMUL_N]
        g = _dot(ua, a_w_in[:, e + c0:e + c0 + MATMUL_N]) + a_b_in_ref[:, e + c0:e + c0 + MATMUL_N]
        h = a * _sigmoid(g)
        for q, j in enumerate(slabs):
            h_ref[j, HALO_A:HALO_A + t, :] = h[:, q * LANES:(q + 1) * LANES]
        z = _dot(ua, a_w_in[:, 2 * e + c0:2 * e + c0 + MATMUL_N])
        z = z + a_b_in_ref[:, 2 * e + c0:2 * e + c0 + MATMUL_N]
        z_ref[:, cols] = _silu(z)
        for j in slabs:
            lanes = slice(j * LANES, (j + 1) * LANES)
            for r0 in range(0, t, CONV_ROWS):
                c_ref[r0:r0 + CONV_ROWS, lanes] = _conv_slab(
                    h_ref, j, first_a, a_cw_ref, a_cb_ref[:, lanes], (r0, CONV_ROWS), k_a)

        ub = ub_ref[...]
        cg = _dot(ub, b_w_in[:, e + c0:e + c0 + MATMUL_N])
        v = _dot(ub, b_w_in[:, 2 * e + c0:2 * e + c0 + MATMUL_N])
        cv = cg * v
        for q, j in enumerate(slabs):
            cv_ref[j, HALO_B:HALO_B + t, :] = cv[:, q * LANES:(q + 1) * LANES]
        bg = _dot(ub, b_w_in[:, c0:c0 + MATMUL_N])
        zb = _dot(ub, b_w_in[:, 3 * e + c0:3 * e + c0 + MATMUL_N])
        t1_ref[:, cols] = bg * _silu(zb)
        for j in slabs:
            lanes = slice(j * LANES, (j + 1) * LANES)
            for r0 in range(0, t, CONV_ROWS):
                hb = _conv_slab(cv_ref, j, first_b, b_cw_ref, None, (r0, CONV_ROWS), k_b)
                gb_ref[r0:r0 + CONV_ROWS, lanes] = (t1_ref[r0:r0 + CONV_ROWS, lanes] * hb).astype(BF16)

    gate_b = mod_b_ref[:, 2 * d:3 * d]
    for r0 in range(0, t, OUT_ROWS):
        rows = slice(r0, r0 + OUT_ROWS)
        rb_ref[rows, :] = rb_ref[rows, :] + gate_b * _dot(gb_ref[rows, :], b_w_out[...])
        for q0 in range(r0, r0 + OUT_ROWS, NORM_ROWS):
            grp = slice(q0, q0 + NORM_ROWS)
            o_ref[grp, :] = _layer_norm(rb_ref[grp, :], b_pg_ref[...], b_pb_ref[...])

    gate_a = mod_a_ref[:, 2 * d:3 * d]
    for r0 in range(0, t, OUT_ROWS):
        rows = slice(r0, r0 + OUT_ROWS)
        for q0 in range(r0, r0 + OUT_ROWS, NORM_ROWS):
            grp = slice(q0, q0 + NORM_ROWS)
            hn = _layer_norm(c_ref[grp, :], a_lg_ref[...], a_lb_ref[...])
            ga_ref[grp, :] = (_silu(hn) * z_ref[grp, :]).astype(BF16)
        ya = _dot(ga_ref[rows, :], a_w_out[...]) + a_b_out_ref[...]
        c_ref[rows, :] = ALPHA * x_ref[rows, :] + gate_a * ya
        for q0 in range(r0, r0 + OUT_ROWS, NORM_ROWS):
            grp = slice(q0, q0 + NORM_ROWS)
            x1_ref[grp, :] = _layer_norm(c_ref[grp, :], a_pg_ref[...], a_pb_ref[...])


def _pack_bf16_rows(w):
    k, n = w.shape
    rows = w.reshape(k // 2, 2, n)
    bits = lambda v: lax.bitcast_convert_type(v.astype(BF16), jnp.uint16).astype(jnp.uint32)
    return bits(rows[:, 0, :]) | (bits(rows[:, 1, :]) << 16)


def _slab_taps(conv_w):
    k_taps, e = conv_w.shape
    return conv_w.reshape(k_taps, e // LANES, LANES).transpose(1, 0, 2)


def _trunk(x, mod, a_w_in, a_b_in, a_conv_w, a_conv_b, a_ln_g, a_ln_b, a_w_out, a_b_out,
           b_w_in, b_conv_w, b_w_out, post_ln_g, post_ln_b):
    b, s, d = x.shape
    e = a_w_out.shape[0]
    k_a, k_b = a_conv_w.shape[0], b_conv_w.shape[0]
    t = SEQ_TILE
    tiles_per_seq = s // t
    n_tiles = b * tiles_per_seq
    n_slabs = e // LANES
    row = lambda v: v.reshape(1, -1)

    def resident(arr):
        return pl.BlockSpec(arr.shape, lambda i: (0,) * arr.ndim, pipeline_mode=pl.Buffered(1))

    a_tile = lambda i: jnp.minimum(i, n_tiles - 1)
    b_tile = lambda i: jnp.maximum(i - 1, 0)
    tile_spec = lambda tile: pl.BlockSpec(
        (None, t, d), lambda i: (tile(i) // tiles_per_seq, tile(i) % tiles_per_seq, 0))
    mod_spec = lambda tile: pl.BlockSpec((None, 1, 3 * d), lambda i: (tile(i) // tiles_per_seq, 0, 0))

    consts = [
        _pack_bf16_rows(a_w_in), row(a_b_in), _slab_taps(a_conv_w), row(a_conv_b),
        row(a_ln_g), row(a_ln_b), _pack_bf16_rows(a_w_out), row(a_b_out),
        row(post_ln_g[0]), row(post_ln_b[0]),
        _pack_bf16_rows(b_w_in), _slab_taps(b_conv_w), _pack_bf16_rows(b_w_out),
        row(post_ln_g[1]), row(post_ln_b[1]),
    ]
    return pl.pallas_call(
        functools.partial(_trunk_kernel, k_a=k_a, k_b=k_b, tiles_per_seq=tiles_per_seq),
        grid=(n_tiles + 1,),
        in_specs=[tile_spec(a_tile), mod_spec(a_tile), mod_spec(b_tile)]
                 + [resident(arr) for arr in consts],
        out_specs=tile_spec(b_tile),
        out_shape=jax.ShapeDtypeStruct((b, s, d), F32),
        scratch_shapes=[
            pltpu.VMEM((t, d), BF16),
            pltpu.VMEM((n_slabs, HALO_A + t, LANES), F32),
            pltpu.VMEM((t, e), F32),
            pltpu.VMEM((t, e), F32),
            pltpu.VMEM((t, e), BF16),
            pltpu.VMEM((t, d), F32),
            pltpu.VMEM((t, d), BF16),
            pltpu.VMEM((n_slabs, HALO_B + t, LANES), F32),
            pltpu.VMEM((t, e), F32),
            pltpu.VMEM((t, e), BF16),
            pltpu.VMEM((t, d), F32),
        ],
        compiler_params=pltpu.CompilerParams(
            dimension_semantics=("arbitrary",),
            vmem_limit_bytes=VMEM_LIMIT),
        name="trunk",
    )(x, mod[0].reshape(b, 1, 3 * d), mod[1].reshape(b, 1, 3 * d), *consts)


def kernel(x, c, ada_w, ada_b, a_w_in, a_b_in, a_conv_w, a_conv_b, a_ln_g, a_ln_b,
           a_w_out, a_b_out, b_w_in, b_conv_w, b_w_out, post_ln_g, post_ln_b):
    assert DEPTH == ada_w.shape[0] == 2 and a_w_in.shape[0] == 1 and b_w_in.shape[0] == 1
    assert a_conv_w.shape[1] - 1 <= HALO_A and b_conv_w.shape[1] - 1 <= HALO_B
    assert x.shape[1] % SEQ_TILE == 0
    mod = _modulation(c, ada_w, ada_b)
    return _trunk(x, mod, a_w_in[0], a_b_in[0], a_conv_w[0], a_conv_b[0], a_ln_g[0], a_ln_b[0],
                  a_w_out[0], a_b_out[0], b_w_in[0], b_conv_w[0], b_w_out[0],
                  post_ln_g, post_ln_b)
```

```python
import functools
import math

import jax
import jax.numpy as jnp
from jax import lax
from jax.experimental import pallas as pl
from jax.experimental.pallas import tpu as pltpu

DEPTH = 2
ALPHA = (2.0 * DEPTH) ** 0.25
LN_EPS = 1e-5
NEG_LOG2_E = -math.log2(math.e)

LANES = 128
SUBLANES = 8
SEQ_TILE = 512
HALO_A = 32
HALO_B = SUBLANES
MATMUL_N = 256
OUT_ROWS = 128
NORM_ROWS = 8
CONV_ROWS = 64
PACK_ROWS = 256
VMEM_LIMIT = 58 * 1024 * 1024

F32 = jnp.float32
BF16 = jnp.bfloat16


def _sigmoid(v):
    return 1.0 / (1.0 + jnp.exp2(v * NEG_LOG2_E))


def _silu(v):
    return v * _sigmoid(v)


def _layer_norm(v, gamma, beta):
    inv_n = 1.0 / v.shape[-1]
    mu = jnp.sum(v, axis=-1, keepdims=True) * inv_n
    dev = v - mu
    var = jnp.sum(dev * dev, axis=-1, keepdims=True) * inv_n
    return dev * lax.rsqrt(var + LN_EPS) * gamma + beta


def _dot(a, b):
    return jnp.dot(a, b, preferred_element_type=F32)


def _mod_kernel(c_ref, w_ref, b_ref, o_ref):
    cond = _silu(c_ref[...])
    o_ref[...] = jnp.dot(cond, w_ref[...], preferred_element_type=F32,
                         precision=lax.Precision.HIGHEST) + b_ref[...]


def _modulation(c, ada_w, ada_b):
    depth, d, n = ada_w.shape
    b = c.shape[0]
    nt = 1024
    return pl.pallas_call(
        _mod_kernel,
        grid=(depth, n // nt),
        in_specs=[
            pl.BlockSpec((b, d), lambda l, j: (0, 0)),
            pl.BlockSpec((None, d, nt), lambda l, j: (l, 0, j)),
            pl.BlockSpec((None, 1, nt), lambda l, j: (l, 0, j)),
        ],
        out_specs=pl.BlockSpec((None, b, nt), lambda l, j: (l, 0, j)),
        out_shape=jax.ShapeDtypeStruct((depth, b, n), F32),
        name="adaln_mod",
    )(c, ada_w, ada_b.reshape(depth, 1, n))


def _conv_slab(src_ref, j, first, w_ref, bias, rows, k_taps):
    r0, n = rows
    acc = w_ref[j, 0:1, :] * src_ref[j, r0 + first:r0 + first + n, :]
    if bias is not None:
        acc = acc + bias
    for k in range(1, k_taps):
        acc = acc + w_ref[j, k:k + 1, :] * src_ref[j, r0 + first + k:r0 + first + k + n, :]
    return acc


def _trunk_kernel(x_ref, mod_a_ref, mod_b_ref,
                  a_w_in_ref, a_b_in_ref, a_cw_ref, a_cb_ref, a_lg_ref, a_lb_ref,
                  a_w_out_ref, a_b_out_ref, a_pg_ref, a_pb_ref,
                  b_w_in_ref, b_cw_ref, b_w_out_ref, b_pg_ref, b_pb_ref,
                  o_ref,
                  ua_ref, h_ref, z_ref, c_ref, ga_ref, x1_ref,
                  ub_ref, cv_ref, t1_ref, gb_ref, rb_ref,
                  *, k_a, k_b, tiles_per_seq):
    i = pl.program_id(0)
    t, d = x_ref.shape
    e = z_ref.shape[1]
    n_slabs = e // LANES
    first_a = HALO_A - (k_a - 1)
    first_b = HALO_B - (k_b - 1)
    a_w_in = a_w_in_ref.bitcast(BF16)
    a_w_out = a_w_out_ref.bitcast(BF16)
    b_w_in = b_w_in_ref.bitcast(BF16)
    b_w_out = b_w_out_ref.bitcast(BF16)

    a_starts = lax.rem(i, tiles_per_seq) == 0
    b_starts = jnp.logical_or(i == 0, lax.rem(i + tiles_per_seq - 1, tiles_per_seq) == 0)

    @pl.when(a_starts)
    def _():
        h_ref[:, 0:HALO_A, :] = jnp.zeros((n_slabs, HALO_A, LANES), F32)

    @pl.when(jnp.logical_not(a_starts))
    def _():
        h_ref[:, 0:HALO_A, :] = h_ref[:, t:t + HALO_A, :]

    @pl.when(b_starts)
    def _():
        cv_ref[:, 0:HALO_B, :] = jnp.zeros((n_slabs, HALO_B, LANES), F32)

    @pl.when(jnp.logical_not(b_starts))
    def _():
        cv_ref[:, 0:HALO_B, :] = cv_ref[:, t:t + HALO_B, :]

    @pl.when(i == 0)
    def _():
        x1_ref[...] = jnp.zeros(x1_ref.shape, F32)

    x1 = x1_ref[...]
    ub_ref[...] = (x1 * (1.0 + mod_b_ref[:, d:2 * d]) + mod_b_ref[:, 0:d]).astype(BF16)
    rb_ref[...] = ALPHA * x1

    ua_ref[...] = (x_ref[...] * (1.0 + mod_a_ref[:, d:2 * d]) + mod_a_ref[:, 0:d]).astype(BF16)

    for c0 in range(0, e, MATMUL_N):
        cols = slice(c0, c0 + MATMUL_N)
        slabs = range(c0 // LANES, (c0 + MATMUL_N) // LANES)

        ua = ua_ref[...]
        a = _dot(ua, a_w_in[:, c0:c0 + MATMUL_N]) + a_b_in_ref[:, c0:c0 + MATMUL_N]
        g = _dot(ua, a_w_in[:, e + c0:e + c0 + MATMUL_N]) + a_b_in_ref[:, e + c0:e + c0 + MATMUL_N]
        h = a * _sigmoid(g)
        for q, j in enumerate(slabs):
            h_ref[j, HALO_A:HALO_A + t, :] = h[:, q * LANES:(q + 1) * LANES]
        z = _dot(ua, a_w_in[:, 2 * e + c0:2 * e + c0 + MATMUL_N])
        z = z + a_b_in_ref[:, 2 * e + c0:2 * e + c0 + MATMUL_N]
        z_ref[:, cols] = _silu(z)
        for j in slabs:
            lanes = slice(j * LANES, (j + 1) * LANES)
            for r0 in range(0, t, CONV_ROWS):
                c_ref[r0:r0 + CONV_ROWS, lanes] = _conv_slab(
                    h_ref, j, first_a, a_cw_ref, a_cb_ref[:, lanes], (r0, CONV_ROWS), k_a)

        ub = ub_ref[...]
        cg = _dot(ub, b_w_in[:, e + c0:e + c0 + MATMUL_N])
        v = _dot(ub, b_w_in[:, 2 * e + c0:2 * e + c0 + MATMUL_N])
        cv = cg * v
        for q, j in enumerate(slabs):
            cv_ref[j, HALO_B:HALO_B + t, :] = cv[:, q * LANES:(q + 1) * LANES]
        bg = _dot(ub, b_w_in[:, c0:c0 + MATMUL_N])
        zb = _dot(ub, b_w_in[:, 3 * e + c0:3 * e + c0 + MATMUL_N])
        t1_ref[:, cols] = bg * _silu(zb)
        for j in slabs:
            lanes = slice(j * LANES, (j + 1) * LANES)
            for r0 in range(0, t, CONV_ROWS):
                hb = _conv_slab(cv_ref, j, first_b, b_cw_ref, None, (r0, CONV_ROWS), k_b)
                gb_ref[r0:r0 + CONV_ROWS, lanes] = (t1_ref[r0:r0 + CONV_ROWS, lanes] * hb).astype(BF16)

    gate_b = mod_b_ref[:, 2 * d:3 * d]
    for r0 in range(0, t, OUT_ROWS):
        rows = slice(r0, r0 + OUT_ROWS)
        rb_ref[rows, :] = rb_ref[rows, :] + gate_b * _dot(gb_ref[rows, :], b_w_out[...])
        for q0 in range(r0, r0 + OUT_ROWS, NORM_ROWS):
            grp = slice(q0, q0 + NORM_ROWS)
            o_ref[grp, :] = _layer_norm(rb_ref[grp, :], b_pg_ref[...], b_pb_ref[...])

    gate_a = mod_a_ref[:, 2 * d:3 * d]
    for r0 in range(0, t, OUT_ROWS):
        rows = slice(r0, r0 + OUT_ROWS)
        for q0 in range(r0, r0 + OUT_ROWS, NORM_ROWS):
            grp = slice(q0, q0 + NORM_ROWS)
            hn = _layer_norm(c_ref[grp, :], a_lg_ref[...], a_lb_ref[...])
            ga_ref[grp, :] = (_silu(hn) * z_ref[grp, :]).astype(BF16)
        ya = _dot(ga_ref[rows, :], a_w_out[...]) + a_b_out_ref[...]
        c_ref[rows, :] = ALPHA * x_ref[rows, :] + gate_a * ya
        for q0 in range(r0, r0 + OUT_ROWS, NORM_ROWS):
            grp = slice(q0, q0 + NORM_ROWS)
            x1_ref[grp, :] = _layer_norm(c_ref[grp, :], a_pg_ref[...], a_pb_ref[...])


def _pack_kernel(w_ref, o_ref):
    o_ref[...] = pltpu.bitcast(w_ref[...].astype(BF16), jnp.uint32)


def _pack_bf16_rows(w):
    k, n = w.shape
    return pl.pallas_call(
        _pack_kernel,
        grid=(k // PACK_ROWS,),
        in_specs=[pl.BlockSpec((PACK_ROWS, n), lambda i: (i, 0))],
        out_specs=pl.BlockSpec((PACK_ROWS // 2, n), lambda i: (i, 0)),
        out_shape=jax.ShapeDtypeStruct((k // 2, n), jnp.uint32),
        name="pack_weights",
    )(w)


def _slab_taps(conv_w):
    k_taps, e = conv_w.shape
    return conv_w.reshape(k_taps, e // LANES, LANES).transpose(1, 0, 2)


def _trunk(x, mod, a_w_in, a_b_in, a_conv_w, a_conv_b, a_ln_g, a_ln_b, a_w_out, a_b_out,
           b_w_in, b_conv_w, b_w_out, post_ln_g, post_ln_b):
    b, s, d = x.shape
    e = a_w_out.shape[0]
    k_a, k_b = a_conv_w.shape[0], b_conv_w.shape[0]
    t = SEQ_TILE
    tiles_per_seq = s // t
    n_tiles = b * tiles_per_seq
    n_slabs = e // LANES
    row = lambda v: v.reshape(1, -1)

    def resident(arr):
        return pl.BlockSpec(arr.shape, lambda i: (0,) * arr.ndim, pipeline_mode=pl.Buffered(1))

    a_tile = lambda i: jnp.minimum(i, n_tiles - 1)
    b_tile = lambda i: jnp.maximum(i - 1, 0)
    tile_spec = lambda tile: pl.BlockSpec(
        (None, t, d), lambda i: (tile(i) // tiles_per_seq, tile(i) % tiles_per_seq, 0))
    mod_spec = lambda tile: pl.BlockSpec((None, 1, 3 * d), lambda i: (tile(i) // tiles_per_seq, 0, 0))

    consts = [
        _pack_bf16_rows(a_w_in), row(a_b_in), _slab_taps(a_conv_w), row(a_conv_b),
        row(a_ln_g), row(a_ln_b), _pack_bf16_rows(a_w_out), row(a_b_out),
        row(post_ln_g[0]), row(post_ln_b[0]),
        _pack_bf16_rows(b_w_in), _slab_taps(b_conv_w), _pack_bf16_rows(b_w_out),
        row(post_ln_g[1]), row(post_ln_b[1]),
    ]
    return pl.pallas_call(
        functools.partial(_trunk_kernel, k_a=k_a, k_b=k_b, tiles_per_seq=tiles_per_seq),
        grid=(n_tiles + 1,),
        in_specs=[tile_spec(a_tile), mod_spec(a_tile), mod_spec(b_tile)]
                 + [resident(arr) for arr in consts],
        out_specs=tile_spec(b_tile),
        out_shape=jax.ShapeDtypeStruct((b, s, d), F32),
        scratch_shapes=[
            pltpu.VMEM((t, d), BF16),
            pltpu.VMEM((n_slabs, HALO_A + t, LANES), F32),
            pltpu.VMEM((t, e), F32),
            pltpu.VMEM((t, e), F32),
            pltpu.VMEM((t, e), BF16),
            pltpu.VMEM((t, d), F32),
            pltpu.VMEM((t, d), BF16),
            pltpu.VMEM((n_slabs, HALO_B + t, LANES), F32),
            pltpu.VMEM((t, e), F32),
            pltpu.VMEM((t, e), BF16),
            pltpu.VMEM((t, d), F32),
        ],
        compiler_params=pltpu.CompilerParams(
            dimension_semantics=("arbitrary",),
            vmem_limit_bytes=VMEM_LIMIT),
        name="trunk",
    )(x, mod[0].reshape(b, 1, 3 * d), mod[1].reshape(b, 1, 3 * d), *consts)


def kernel(x, c, ada_w, ada_b, a_w_in, a_b_in, a_conv_w, a_conv_b, a_ln_g, a_ln_b,
           a_w_out, a_b_out, b_w_in, b_conv_w, b_w_out, post_ln_g, post_ln_b):
    assert DEPTH == ada_w.shape[0] == 2 and a_w_in.shape[0] == 1 and b_w_in.shape[0] == 1
    assert a_conv_w.shape[1] - 1 <= HALO_A and b_conv_w.shape[1] - 1 <= HALO_B
    assert x.shape[1] % SEQ_TILE == 0
    mod = _modulation(c, ada_w, ada_b)
    return _trunk(x, mod, a_w_in[0], a_b_in[0], a_conv_w[0], a_conv_b[0], a_ln_g[0], a_ln_b[0],
                  a_w_out[0], a_b_out[0], b_w_in[0], b_conv_w[0], b_w_out[0],
                  post_ln_g, post_ln_b)
```

```python
import functools
import math

import jax
import jax.numpy as jnp
from jax import lax
from jax.experimental import pallas as pl
from jax.experimental.pallas import tpu as pltpu

DEPTH = 2
ALPHA = (2.0 * DEPTH) ** 0.25
LN_EPS = 1e-5
NEG_LOG2_E = -math.log2(math.e)

LANES = 128
SUBLANES = 8
SEQ_TILE = 512
HALO_A = 32
HALO_B = SUBLANES
MATMUL_N = 256
OUT_ROWS = 128
NORM_ROWS = 8
CONV_ROWS = 32
PACK_ROWS = 256
VMEM_LIMIT = 58 * 1024 * 1024

F32 = jnp.float32
BF16 = jnp.bfloat16


def _sigmoid(v):
    return 1.0 / (1.0 + jnp.exp2(v * NEG_LOG2_E))


def _silu(v):
    return v * _sigmoid(v)


def _layer_norm(v, gamma, beta):
    inv_n = 1.0 / v.shape[-1]
    mu = jnp.sum(v, axis=-1, keepdims=True) * inv_n
    dev = v - mu
    var = jnp.sum(dev * dev, axis=-1, keepdims=True) * inv_n
    return dev * lax.rsqrt(var + LN_EPS) * gamma + beta


def _dot(a, b):
    return jnp.dot(a, b, preferred_element_type=F32)


def _mod_kernel(c_ref, w_ref, b_ref, o_ref):
    cond = _silu(c_ref[...])
    o_ref[...] = jnp.dot(cond, w_ref[...], preferred_element_type=F32,
                         precision=lax.Precision.HIGHEST) + b_ref[...]


def _modulation(c, ada_w, ada_b):
    depth, d, n = ada_w.shape
    b = c.shape[0]
    nt = 1024
    return pl.pallas_call(
        _mod_kernel,
        grid=(depth, n // nt),
        in_specs=[
            pl.BlockSpec((b, d), lambda l, j: (0, 0)),
            pl.BlockSpec((None, d, nt), lambda l, j: (l, 0, j)),
            pl.BlockSpec((None, 1, nt), lambda l, j: (l, 0, j)),
        ],
        out_specs=pl.BlockSpec((None, b, nt), lambda l, j: (l, 0, j)),
        out_shape=jax.ShapeDtypeStruct((depth, b, n), F32),
        name="adaln_mod",
    )(c, ada_w, ada_b.reshape(depth, 1, n))


def _conv_slab(src_ref, j, first, w_ref, bias, rows, k_taps):
    r0, n = rows
    acc = w_ref[j, 0:1, :] * src_ref[j, r0 + first:r0 + first + n, :]
    if bias is not None:
        acc = acc + bias
    for k in range(1, k_taps):
        acc = acc + w_ref[j, k:k + 1, :] * src_ref[j, r0 + first + k:r0 + first + k + n, :]
    return acc


def _trunk_kernel(x_ref, mod_a_ref, mod_b_ref,
                  a_w_in_ref, a_b_in_ref, a_cw_ref, a_cb_ref, a_lg_ref, a_lb_ref,
                  a_w_out_ref, a_b_out_ref, a_pg_ref, a_pb_ref,
                  b_w_in_ref, b_cw_ref, b_w_out_ref, b_pg_ref, b_pb_ref,
                  o_ref,
                  ua_ref, h_ref, z_ref, c_ref, ga_ref, x1_ref,
                  ub_ref, cv_ref, t1_ref, gb_ref, rb_ref,
                  *, k_a, k_b, tiles_per_seq):
    i = pl.program_id(0)
    t, d = x_ref.shape
    e = z_ref.shape[1]
    n_slabs = e // LANES
    first_a = HALO_A - (k_a - 1)
    first_b = HALO_B - (k_b - 1)
    a_w_in = a_w_in_ref.bitcast(BF16)
    a_w_out = a_w_out_ref.bitcast(BF16)
    b_w_in = b_w_in_ref.bitcast(BF16)
    b_w_out = b_w_out_ref.bitcast(BF16)

    a_starts = lax.rem(i, tiles_per_seq) == 0
    b_starts = jnp.logical_or(i == 0, lax.rem(i + tiles_per_seq - 1, tiles_per_seq) == 0)

    @pl.when(a_starts)
    def _():
        h_ref[:, 0:HALO_A, :] = jnp.zeros((n_slabs, HALO_A, LANES), F32)

    @pl.when(jnp.logical_not(a_starts))
    def _():
        h_ref[:, 0:HALO_A, :] = h_ref[:, t:t + HALO_A, :]

    @pl.when(b_starts)
    def _():
        cv_ref[:, 0:HALO_B, :] = jnp.zeros((n_slabs, HALO_B, LANES), F32)

    @pl.when(jnp.logical_not(b_starts))
    def _():
        cv_ref[:, 0:HALO_B, :] = cv_ref[:, t:t + HALO_B, :]

    @pl.when(i == 0)
    def _():
        x1_ref[...] = jnp.zeros(x1_ref.shape, F32)

    x1 = x1_ref[...]
    ub_ref[...] = (x1 * (1.0 + mod_b_ref[:, d:2 * d]) + mod_b_ref[:, 0:d]).astype(BF16)

    ua_ref[...] = (x_ref[...] * (1.0 + mod_a_ref[:, d:2 * d]) + mod_a_ref[:, 0:d]).astype(BF16)

    for c0 in range(0, e, MATMUL_N):
        cols = slice(c0, c0 + MATMUL_N)
        slabs = range(c0 // LANES, (c0 + MATMUL_N) // LANES)

        ua = ua_ref[...]
        a = _dot(ua, a_w_in[:, c0:c0 + MATMUL_N]) + a_b_in_ref[:, c0:c0 + MATMUL_N]
        g = _dot(ua, a_w_in[:, e + c0:e + c0 + MATMUL_N]) + a_b_in_ref[:, e + c0:e + c0 + MATMUL_N]
        h = a * _sigmoid(g)
        for q, j in enumerate(slabs):
            h_ref[j, HALO_A:HALO_A + t, :] = h[:, q * LANES:(q + 1) * LANES]
        z = _dot(ua, a_w_in[:, 2 * e + c0:2 * e + c0 + MATMUL_N])
        z = z + a_b_in_ref[:, 2 * e + c0:2 * e + c0 + MATMUL_N]
        z_ref[:, cols] = _silu(z)
        for j in slabs:
            lanes = slice(j * LANES, (j + 1) * LANES)
            for r0 in range(0, t, CONV_ROWS):
                c_ref[r0:r0 + CONV_ROWS, lanes] = _conv_slab(
                    h_ref, j, first_a, a_cw_ref, a_cb_ref[:, lanes], (r0, CONV_ROWS), k_a)

        ub = ub_ref[...]
        cg = _dot(ub, b_w_in[:, e + c0:e + c0 + MATMUL_N])
        v = _dot(ub, b_w_in[:, 2 * e + c0:2 * e + c0 + MATMUL_N])
        cv = cg * v
        for q, j in enumerate(slabs):
            cv_ref[j, HALO_B:HALO_B + t, :] = cv[:, q * LANES:(q + 1) * LANES]
        bg = _dot(ub, b_w_in[:, c0:c0 + MATMUL_N])
        zb = _dot(ub, b_w_in[:, 3 * e + c0:3 * e + c0 + MATMUL_N])
        t1_ref[:, cols] = bg * _silu(zb)
        for j in slabs:
            lanes = slice(j * LANES, (j + 1) * LANES)
            for r0 in range(0, t, CONV_ROWS):
                hb = _conv_slab(cv_ref, j, first_b, b_cw_ref, None, (r0, CONV_ROWS), k_b)
                gb_ref[r0:r0 + CONV_ROWS, lanes] = (t1_ref[r0:r0 + CONV_ROWS, lanes] * hb).astype(BF16)

    gate_b = mod_b_ref[:, 2 * d:3 * d]
    for r0 in range(0, t, OUT_ROWS):
        rows = slice(r0, r0 + OUT_ROWS)
        rb_ref[rows, :] = ALPHA * x1_ref[rows, :] + gate_b * _dot(gb_ref[rows, :], b_w_out[...])
        for q0 in range(r0, r0 + OUT_ROWS, NORM_ROWS):
            grp = slice(q0, q0 + NORM_ROWS)
            o_ref[grp, :] = _layer_norm(rb_ref[grp, :], b_pg_ref[...], b_pb_ref[...])

    gate_a = mod_a_ref[:, 2 * d:3 * d]
    for r0 in range(0, t, OUT_ROWS):
        rows = slice(r0, r0 + OUT_ROWS)
        for q0 in range(r0, r0 + OUT_ROWS, NORM_ROWS):
            grp = slice(q0, q0 + NORM_ROWS)
            hn = _layer_norm(c_ref[grp, :], a_lg_ref[...], a_lb_ref[...])
            ga_ref[grp, :] = (_silu(hn) * z_ref[grp, :]).astype(BF16)
        ya = _dot(ga_ref[rows, :], a_w_out[...]) + a_b_out_ref[...]
        c_ref[rows, :] = ALPHA * x_ref[rows, :] + gate_a * ya
        for q0 in range(r0, r0 + OUT_ROWS, NORM_ROWS):
            grp = slice(q0, q0 + NORM_ROWS)
            x1_ref[grp, :] = _layer_norm(c_ref[grp, :], a_pg_ref[...], a_pb_ref[...])


def _pack_kernel(w_ref, o_ref):
    o_ref[...] = pltpu.bitcast(w_ref[...].astype(BF16), jnp.uint32)


def _pack_bf16_rows(w):
    k, n = w.shape
    return pl.pallas_call(
        _pack_kernel,
        grid=(k // PACK_ROWS,),
        in_specs=[pl.BlockSpec((PACK_ROWS, n), lambda i: (i, 0))],
        out_specs=pl.BlockSpec((PACK_ROWS // 2, n), lambda i: (i, 0)),
        out_shape=jax.ShapeDtypeStruct((k // 2, n), jnp.uint32),
        name="pack_weights",
    )(w)


def _slab_taps(conv_w):
    k_taps, e = conv_w.shape
    return conv_w.reshape(k_taps, e // LANES, LANES).transpose(1, 0, 2)


def _trunk(x, mod, a_w_in, a_b_in, a_conv_w, a_conv_b, a_ln_g, a_ln_b, a_w_out, a_b_out,
           b_w_in, b_conv_w, b_w_out, post_ln_g, post_ln_b):
    b, s, d = x.shape
    e = a_w_out.shape[0]
    k_a, k_b = a_conv_w.shape[0], b_conv_w.shape[0]
    t = SEQ_TILE
    tiles_per_seq = s // t
    n_tiles = b * tiles_per_seq
    n_slabs = e // LANES
    row = lambda v: v.reshape(1, -1)

    def resident(arr):
        return pl.BlockSpec(arr.shape, lambda i: (0,) * arr.ndim, pipeline_mode=pl.Buffered(1))

    a_tile = lambda i: jnp.minimum(i, n_tiles - 1)
    b_tile = lambda i: jnp.maximum(i - 1, 0)
    tile_spec = lambda tile: pl.BlockSpec(
        (None, t, d), lambda i: (tile(i) // tiles_per_seq, tile(i) % tiles_per_seq, 0))
    mod_spec = lambda tile: pl.BlockSpec((None, 1, 3 * d), lambda i: (tile(i) // tiles_per_seq, 0, 0))

    consts = [
        _pack_bf16_rows(a_w_in), row(a_b_in), _slab_taps(a_conv_w), row(a_conv_b),
        row(a_ln_g), row(a_ln_b), _pack_bf16_rows(a_w_out), row(a_b_out),
        row(post_ln_g[0]), row(post_ln_b[0]),
        _pack_bf16_rows(b_w_in), _slab_taps(b_conv_w), _pack_bf16_rows(b_w_out),
        row(post_ln_g[1]), row(post_ln_b[1]),
    ]
    return pl.pallas_call(
        functools.partial(_trunk_kernel, k_a=k_a, k_b=k_b, tiles_per_seq=tiles_per_seq),
        grid=(n_tiles + 1,),
        in_specs=[tile_spec(a_tile), mod_spec(a_tile), mod_spec(b_tile)]
                 + [resident(arr) for arr in consts],
        out_specs=tile_spec(b_tile),
        out_shape=jax.ShapeDtypeStruct((b, s, d), F32),
        scratch_shapes=[
            pltpu.VMEM((t, d), BF16),
            pltpu.VMEM((n_slabs, HALO_A + t, LANES), F32),
            pltpu.VMEM((t, e), F32),
            pltpu.VMEM((t, e), F32),
            pltpu.VMEM((t, e), BF16),
            pltpu.VMEM((t, d), F32),
            pltpu.VMEM((t, d), BF16),
            pltpu.VMEM((n_slabs, HALO_B + t, LANES), F32),
            pltpu.VMEM((t, e), F32),
            pltpu.VMEM((t, e), BF16),
            pltpu.VMEM((t, d), F32),
        ],
        compiler_params=pltpu.CompilerParams(
            dimension_semantics=("arbitrary",),
            vmem_limit_bytes=VMEM_LIMIT),
        name="trunk",
    )(x, mod[0].reshape(b, 1, 3 * d), mod[1].reshape(b, 1, 3 * d), *consts)


def kernel(x, c, ada_w, ada_b, a_w_in, a_b_in, a_conv_w, a_conv_b, a_ln_g, a_ln_b,
           a_w_out, a_b_out, b_w_in, b_conv_w, b_w_out, post_ln_g, post_ln_b):
    assert DEPTH == ada_w.shape[0] == 2 and a_w_in.shape[0] == 1 and b_w_in.shape[0] == 1
    assert a_conv_w.shape[1] - 1 <= HALO_A and b_conv_w.shape[1] - 1 <= HALO_B
    assert x.shape[1] % SEQ_TILE == 0
    mod = _modulation(c, ada_w, ada_b)
    return _trunk(x, mod, a_w_in[0], a_b_in[0], a_conv_w[0], a_conv_b[0], a_ln_g[0], a_ln_b[0],
                  a_w_out[0], a_b_out[0], b_w_in[0], b_conv_w[0], b_w_out[0],
                  post_ln_g, post_ln_b)
```

```python
import functools
import math

import jax
import jax.numpy as jnp
from jax import lax
from jax.experimental import pallas as pl
from jax.experimental.pallas import tpu as pltpu

DEPTH = 2
ALPHA = (2.0 * DEPTH) ** 0.25
LN_EPS = 1e-5
NEG_LOG2_E = -math.log2(math.e)

LANES = 128
SUBLANES = 8
SEQ_TILE = 512
HALO_A = 32
HALO_B = SUBLANES
MATMUL_N = 256
OUT_ROWS = 128
NORM_ROWS = 16
CONV_ROWS = 32
PACK_ROWS = 256
VMEM_LIMIT = 58 * 1024 * 1024

F32 = jnp.float32
BF16 = jnp.bfloat16


def _sigmoid(v):
    return 1.0 / (1.0 + jnp.exp2(v * NEG_LOG2_E))


def _silu(v):
    return v * _sigmoid(v)


def _layer_norm(v, gamma, beta):
    inv_n = 1.0 / v.shape[-1]
    mu = jnp.sum(v, axis=-1, keepdims=True) * inv_n
    dev = v - mu
    var = jnp.sum(dev * dev, axis=-1, keepdims=True) * inv_n
    return dev * lax.rsqrt(var + LN_EPS) * gamma + beta


def _dot(a, b):
    return jnp.dot(a, b, preferred_element_type=F32)


def _mod_kernel(c_ref, w_ref, b_ref, o_ref):
    cond = _silu(c_ref[...])
    o_ref[...] = jnp.dot(cond, w_ref[...], preferred_element_type=F32,
                         precision=lax.Precision.HIGHEST) + b_ref[...]


def _modulation(c, ada_w, ada_b):
    depth, d, n = ada_w.shape
    b = c.shape[0]
    nt = 1024
    return pl.pallas_call(
        _mod_kernel,
        grid=(depth, n // nt),
        in_specs=[
            pl.BlockSpec((b, d), lambda l, j: (0, 0)),
            pl.BlockSpec((None, d, nt), lambda l, j: (l, 0, j)),
            pl.BlockSpec((None, 1, nt), lambda l, j: (l, 0, j)),
        ],
        out_specs=pl.BlockSpec((None, b, nt), lambda l, j: (l, 0, j)),
        out_shape=jax.ShapeDtypeStruct((depth, b, n), F32),
        name="adaln_mod",
    )(c, ada_w, ada_b.reshape(depth, 1, n))


def _conv_slab(src_ref, j, first, w_ref, bias, rows, k_taps):
    r0, n = rows
    acc = w_ref[j, 0:1, :] * src_ref[j, r0 + first:r0 + first + n, :]
    if bias is not None:
        acc = acc + bias
    for k in range(1, k_taps):
        acc = acc + w_ref[j, k:k + 1, :] * src_ref[j, r0 + first + k:r0 + first + k + n, :]
    return acc


def _trunk_kernel(x_ref, mod_a_ref, mod_b_ref,
                  a_w_in_ref, a_b_in_ref, a_cw_ref, a_cb_ref, a_lg_ref, a_lb_ref,
                  a_w_out_ref, a_b_out_ref, a_pg_ref, a_pb_ref,
                  b_w_in_ref, b_cw_ref, b_w_out_ref, b_pg_ref, b_pb_ref,
                  o_ref,
                  ua_ref, h_ref, z_ref, c_ref, ga_ref, x1_ref,
                  ub_ref, cv_ref, t1_ref, gb_ref, rb_ref,
                  *, k_a, k_b, tiles_per_seq):
    i = pl.program_id(0)
    t, d = x_ref.shape
    e = z_ref.shape[1]
    n_slabs = e // LANES
    first_a = HALO_A - (k_a - 1)
    first_b = HALO_B - (k_b - 1)
    a_w_in = a_w_in_ref.bitcast(BF16)
    a_w_out = a_w_out_ref.bitcast(BF16)
    b_w_in = b_w_in_ref.bitcast(BF16)
    b_w_out = b_w_out_ref.bitcast(BF16)

    a_starts = lax.rem(i, tiles_per_seq) == 0
    b_starts = jnp.logical_or(i == 0, lax.rem(i + tiles_per_seq - 1, tiles_per_seq) == 0)

    @pl.when(a_starts)
    def _():
        h_ref[:, 0:HALO_A, :] = jnp.zeros((n_slabs, HALO_A, LANES), F32)

    @pl.when(jnp.logical_not(a_starts))
    def _():
        h_ref[:, 0:HALO_A, :] = h_ref[:, t:t + HALO_A, :]

    @pl.when(b_starts)
    def _():
        cv_ref[:, 0:HALO_B, :] = jnp.zeros((n_slabs, HALO_B, LANES), F32)

    @pl.when(jnp.logical_not(b_starts))
    def _():
        cv_ref[:, 0:HALO_B, :] = cv_ref[:, t:t + HALO_B, :]

    @pl.when(i == 0)
    def _():
        x1_ref[...] = jnp.zeros(x1_ref.shape, F32)

    x1 = x1_ref[...]
    ub_ref[...] = (x1 * (1.0 + mod_b_ref[:, d:2 * d]) + mod_b_ref[:, 0:d]).astype(BF16)

    ua_ref[...] = (x_ref[...] * (1.0 + mod_a_ref[:, d:2 * d]) + mod_a_ref[:, 0:d]).astype(BF16)

    for c0 in range(0, e, MATMUL_N):
        cols = slice(c0, c0 + MATMUL_N)
        slabs = range(c0 // LANES, (c0 + MATMUL_N) // LANES)

        ua = ua_ref[...]
        a = _dot(ua, a_w_in[:, c0:c0 + MATMUL_N]) + a_b_in_ref[:, c0:c0 + MATMUL_N]
        g = _dot(ua, a_w_in[:, e + c0:e + c0 + MATMUL_N]) + a_b_in_ref[:, e + c0:e + c0 + MATMUL_N]
        h = a * _sigmoid(g)
        for q, j in enumerate(slabs):
            h_ref[j, HALO_A:HALO_A + t, :] = h[:, q * LANES:(q + 1) * LANES]
        z = _dot(ua, a_w_in[:, 2 * e + c0:2 * e + c0 + MATMUL_N])
        z = z + a_b_in_ref[:, 2 * e + c0:2 * e + c0 + MATMUL_N]
        z_ref[:, cols] = _silu(z)
        for j in slabs:
            lanes = slice(j * LANES, (j + 1) * LANES)
            for r0 in range(0, t, CONV_ROWS):
                c_ref[r0:r0 + CONV_ROWS, lanes] = _conv_slab(
                    h_ref, j, first_a, a_cw_ref, a_cb_ref[:, lanes], (r0, CONV_ROWS), k_a)

        ub = ub_ref[...]
        cg = _dot(ub, b_w_in[:, e + c0:e + c0 + MATMUL_N])
        v = _dot(ub, b_w_in[:, 2 * e + c0:2 * e + c0 + MATMUL_N])
        cv = cg * v
        for q, j in enumerate(slabs):
            cv_ref[j, HALO_B:HALO_B + t, :] = cv[:, q * LANES:(q + 1) * LANES]
        bg = _dot(ub, b_w_in[:, c0:c0 + MATMUL_N])
        zb = _dot(ub, b_w_in[:, 3 * e + c0:3 * e + c0 + MATMUL_N])
        t1_ref[:, cols] = bg * _silu(zb)
        for j in slabs:
            lanes = slice(j * LANES, (j + 1) * LANES)
            for r0 in range(0, t, CONV_ROWS):
                hb = _conv_slab(cv_ref, j, first_b, b_cw_ref, None, (r0, CONV_ROWS), k_b)
                gb_ref[r0:r0 + CONV_ROWS, lanes] = (t1_ref[r0:r0 + CONV_ROWS, lanes] * hb).astype(BF16)

    gate_b = mod_b_ref[:, 2 * d:3 * d]
    for r0 in range(0, t, OUT_ROWS):
        rows = slice(r0, r0 + OUT_ROWS)
        rb_ref[rows, :] = ALPHA * x1_ref[rows, :] + gate_b * _dot(gb_ref[rows, :], b_w_out[...])
        for q0 in range(r0, r0 + OUT_ROWS, NORM_ROWS):
            grp = slice(q0, q0 + NORM_ROWS)
            o_ref[grp, :] = _layer_norm(rb_ref[grp, :], b_pg_ref[...], b_pb_ref[...])

    gate_a = mod_a_ref[:, 2 * d:3 * d]
    for r0 in range(0, t, OUT_ROWS):
        rows = slice(r0, r0 + OUT_ROWS)
        for q0 in range(r0, r0 + OUT_ROWS, NORM_ROWS):
            grp = slice(q0, q0 + NORM_ROWS)
            hn = _layer_norm(c_ref[grp, :], a_lg_ref[...], a_lb_ref[...])
            ga_ref[grp, :] = (_silu(hn) * z_ref[grp, :]).astype(BF16)
        ya = _dot(ga_ref[rows, :], a_w_out[...]) + a_b_out_ref[...]
        c_ref[rows, :] = ALPHA * x_ref[rows, :] + gate_a * ya
        for q0 in range(r0, r0 + OUT_ROWS, NORM_ROWS):
            grp = slice(q0, q0 + NORM_ROWS)
            x1_ref[grp, :] = _layer_norm(c_ref[grp, :], a_pg_ref[...], a_pb_ref[...])


def _pack_kernel(w_ref, o_ref):
    o_ref[...] = pltpu.bitcast(w_ref[...].astype(BF16), jnp.uint32)


def _pack_bf16_rows(w):
    k, n = w.shape
    return pl.pallas_call(
        _pack_kernel,
        grid=(k // PACK_ROWS,),
        in_specs=[pl.BlockSpec((PACK_ROWS, n), lambda i: (i, 0))],
        out_specs=pl.BlockSpec((PACK_ROWS // 2, n), lambda i: (i, 0)),
        out_shape=jax.ShapeDtypeStruct((k // 2, n), jnp.uint32),
        name="pack_weights",
    )(w)


def _slab_taps(conv_w):
    k_taps, e = conv_w.shape
    return conv_w.reshape(k_taps, e // LANES, LANES).transpose(1, 0, 2)


def _trunk(x, mod, a_w_in, a_b_in, a_conv_w, a_conv_b, a_ln_g, a_ln_b, a_w_out, a_b_out,
           b_w_in, b_conv_w, b_w_out, post_ln_g, post_ln_b):
    b, s, d = x.shape
    e = a_w_out.shape[0]
    k_a, k_b = a_conv_w.shape[0], b_conv_w.shape[0]
    t = SEQ_TILE
    tiles_per_seq = s // t
    n_tiles = b * tiles_per_seq
    n_slabs = e // LANES
    row = lambda v: v.reshape(1, -1)

    def resident(arr):
        return pl.BlockSpec(arr.shape, lambda i: (0,) * arr.ndim, pipeline_mode=pl.Buffered(1))

    a_tile = lambda i: jnp.minimum(i, n_tiles - 1)
    b_tile = lambda i: jnp.maximum(i - 1, 0)
    tile_spec = lambda tile: pl.BlockSpec(
        (None, t, d), lambda i: (tile(i) // tiles_per_seq, tile(i) % tiles_per_seq, 0))
    mod_spec = lambda tile: pl.BlockSpec((None, 1, 3 * d), lambda i: (tile(i) // tiles_per_seq, 0, 0))

    consts = [
        _pack_bf16_rows(a_w_in), row(a_b_in), _slab_taps(a_conv_w), row(a_conv_b),
        row(a_ln_g), row(a_ln_b), _pack_bf16_rows(a_w_out), row(a_b_out),
        row(post_ln_g[0]), row(post_ln_b[0]),
        _pack_bf16_rows(b_w_in), _slab_taps(b_conv_w), _pack_bf16_rows(b_w_out),
        row(post_ln_g[1]), row(post_ln_b[1]),
    ]
    return pl.pallas_call(
        functools.partial(_trunk_kernel, k_a=k_a, k_b=k_b, tiles_per_seq=tiles_per_seq),
        grid=(n_tiles + 1,),
        in_specs=[tile_spec(a_tile), mod_spec(a_tile), mod_spec(b_tile)]
                 + [resident(arr) for arr in consts],
        out_specs=tile_spec(b_tile),
        out_shape=jax.ShapeDtypeStruct((b, s, d), F32),
        scratch_shapes=[
            pltpu.VMEM((t, d), BF16),
            pltpu.VMEM((n_slabs, HALO_A + t, LANES), F32),
            pltpu.VMEM((t, e), F32),
            pltpu.VMEM((t, e), F32),
            pltpu.VMEM((t, e), BF16),
            pltpu.VMEM((t, d), F32),
            pltpu.VMEM((t, d), BF16),
            pltpu.VMEM((n_slabs, HALO_B + t, LANES), F32),
            pltpu.VMEM((t, e), F32),
            pltpu.VMEM((t, e), BF16),
            pltpu.VMEM((t, d), F32),
        ],
        compiler_params=pltpu.CompilerParams(
            dimension_semantics=("arbitrary",),
            vmem_limit_bytes=VMEM_LIMIT),
        name="trunk",
    )(x, mod[0].reshape(b, 1, 3 * d), mod[1].reshape(b, 1, 3 * d), *consts)


def kernel(x, c, ada_w, ada_b, a_w_in, a_b_in, a_conv_w, a_conv_b, a_ln_g, a_ln_b,
           a_w_out, a_b_out, b_w_in, b_conv_w, b_w_out, post_ln_g, post_ln_b):
    assert DEPTH == ada_w.shape[0] == 2 and a_w_in.shape[0] == 1 and b_w_in.shape[0] == 1
    assert a_conv_w.shape[1] - 1 <= HALO_A and b_conv_w.shape[1] - 1 <= HALO_B
    assert x.shape[1] % SEQ_TILE == 0
    mod = _modulation(c, ada_w, ada_b)
    return _trunk(x, mod, a_w_in[0], a_b_in[0], a_conv_w[0], a_conv_b[0], a_ln_g[0], a_ln_b[0],
                  a_w_out[0], a_b_out[0], b_w_in[0], b_conv_w[0], b_w_out[0],
                  post_ln_g, post_ln_b)
```
